```python
import jax, jax.numpy as jnp
from jax import lax
import numpy as np

D_MODEL = 1024
BATCH = 8
SEQ = 4096
DEPTH = 1

W_A = 1024
W_B = 1024
K_A = 3
K_B = 31
N_GROUPS_A = 16
N_GROUPS_B = 16
EPS = 1e-6
SPLIT_SIZES = (W_A, W_A, W_A, W_A, W_B, W_B, W_B, D_MODEL, D_MODEL)
D_IN = sum(SPLIT_SIZES)

kernel_name = "hybrid_shortconv_conformer_gated_block"


def rms_norm(x, gain):
    xf = x.astype(jnp.float32)
    y = xf * lax.rsqrt(jnp.mean(xf * xf, axis=-1, keepdims=True) + EPS)
    return (y * gain.astype(jnp.float32)).astype(x.dtype)


def layer_norm(x, gain, bias):
    xf = x.astype(jnp.float32)
    mu = jnp.mean(xf, axis=-1, keepdims=True)
    var = jnp.mean(jnp.square(xf - mu), axis=-1, keepdims=True)
    y = (xf - mu) * lax.rsqrt(var + EPS)
    return (y * gain.astype(jnp.float32) + bias.astype(jnp.float32)).astype(x.dtype)


def depthwise_conv_centred(u, w):
    k, ch = w.shape
    pad = (k - 1) // 2
    return lax.conv_general_dilated(
        u, w[:, None, :].astype(u.dtype), window_strides=(1,), padding=[(pad, pad)],
        dimension_numbers=("NWC", "WIO", "NWC"), feature_group_count=ch)


def split_columns(p):
    idx = np.cumsum(np.array(SPLIT_SIZES))[:-1].tolist()
    return jnp.split(p, idx, axis=-1)


def setup_inputs(seed: int = 0) -> dict:
    key = jax.random.key(seed)
    ks = jax.random.split(key, 20)
    f32 = jnp.float32

    def nrm(k, shape, scale):
        return jax.random.normal(k, shape, f32) * scale

    L, D = DEPTH, D_MODEL
    return {
        "x": nrm(ks[0], (BATCH, SEQ, D), 1.0),
        "c": nrm(ks[1], (BATCH, D), 1.0),
        "norm_gain": 1.0 + nrm(ks[2], (L, D), 0.02),
        "w_ada": nrm(ks[3], (L, D, 3 * D), 0.5 * D ** -0.5),
        "b_ada": nrm(ks[4], (L, 3 * D), 0.02),
        "w_in": nrm(ks[5], (L, D, D_IN), D ** -0.5),
        "b_merge": nrm(ks[6], (L, 2 * D), 0.02),
        "conv_a_w": nrm(ks[7], (L, K_A, W_A), K_A ** -0.5),
        "w_out_a": nrm(ks[8], (L, W_A, D), W_A ** -0.5),
        "conv_b_w": nrm(ks[9], (L, K_B, W_B), K_B ** -0.5),
        "conv_b_bias": nrm(ks[10], (L, W_B), 0.02),
        "ln_b_gain": 1.0 + nrm(ks[11], (L, W_B), 0.02),
        "ln_b_bias": nrm(ks[12], (L, W_B), 0.02),
        "w_out_b": nrm(ks[13], (L, W_B, D), W_B ** -0.5),
        "b_out_b": nrm(ks[14], (L, D), 0.02),
        "w_o": nrm(ks[15], (L, D, D), D ** -0.5),
        "final_gain": 1.0 + nrm(ks[16], (D,), 0.02),
    }


def reference(x, c, norm_gain, w_ada, b_ada, w_in, b_merge, conv_a_w, w_out_a,
              conv_b_w, conv_b_bias, ln_b_gain, ln_b_bias, w_out_b, b_out_b, w_o,
              final_gain):
    c_act = jax.nn.silu(c)
    for l in range(DEPTH):
        mod = c_act @ w_ada[l] + b_ada[l]
        shift, scale, gate = jnp.split(mod, 3, axis=-1)
        h = rms_norm(x, norm_gain[l]) * (1.0 + scale[:, None, :]) + shift[:, None, :]

        proj = h @ w_in[l]
        b_a, c_a, v_a, z_a, a_b, g_b, z_b, m_a, m_b = split_columns(proj)
        bm_a, bm_b = jnp.split(b_merge[l], 2, axis=-1)

        y_a = b_a * depthwise_conv_centred(c_a * v_a, conv_a_w[l])
        y_a = (y_a * jax.nn.silu(z_a)) @ w_out_a[l]

        u_b = a_b * jax.nn.sigmoid(g_b)
        u_b = depthwise_conv_centred(u_b, conv_b_w[l]) + conv_b_bias[l]
        u_b = jax.nn.silu(layer_norm(u_b, ln_b_gain[l], ln_b_bias[l]))
        y_b = (u_b * jax.nn.silu(z_b)) @ w_out_b[l] + b_out_b[l]

        merged = jax.nn.sigmoid(m_a + bm_a) * y_a + jax.nn.sigmoid(m_b + bm_b) * y_b
        x = x + gate[:, None, :] * (merged @ w_o[l])
    return rms_norm(x, final_gain)
```

```python
import functools

import jax
import jax.numpy as jnp
from jax import lax
from jax.experimental import pallas as pl
from jax.experimental.pallas import tpu as pltpu

EPS = 1e-6
K_A = 3
K_B = 31
N_COL_GROUPS = 9

LANES = 128
BF16_SUBLANES = 16
TOKENS_PER_STEP = 512
HALO = 16
CHUNK = 256
SLABS_PER_CHUNK = CHUNK // LANES
CONV_ROWS = 64
ROW_BLOCK = 32
VMEM_LIMIT_BYTES = 60 * 1024 * 1024

assert HALO >= (K_B - 1) // 2 and HALO % BF16_SUBLANES == 0
assert ROW_BLOCK % BF16_SUBLANES == 0 and TOKENS_PER_STEP % ROW_BLOCK == 0
assert TOKENS_PER_STEP % CONV_ROWS == 0


def _sigmoid(v):
    return jax.nn.sigmoid(v)


def _dot(a, b):
    return jnp.dot(a, b, preferred_element_type=jnp.float32)


def _ada_kernel(c_ref, w_ref, b_ref, o_ref):
    c = c_ref[...]
    act = c * _sigmoid(c)
    o_ref[...] = jnp.dot(act, w_ref[...], preferred_element_type=jnp.float32,
                         precision=lax.Precision.HIGHEST) + b_ref[...]


def _ada_modulation(c, w_ada, b_ada):
    batch, d = c.shape
    n_out = w_ada.shape[1]
    return pl.pallas_call(
        _ada_kernel,
        grid=(n_out // d,),
        in_specs=[
            pl.BlockSpec((batch, d), lambda j: (0, 0)),
            pl.BlockSpec((d, d), lambda j: (0, j)),
            pl.BlockSpec((1, d), lambda j: (0, j)),
        ],
        out_specs=pl.BlockSpec((batch, d), lambda j: (0, j)),
        out_shape=jax.ShapeDtypeStruct((batch, n_out), jnp.float32),
        name="ada_modulation",
    )(c, w_ada, b_ada.reshape(1, n_out))


def _block_kernel(xp_ref, xm_ref, xn_ref, mod_ref, ng_ref, win_ref, bm_ref, cwa_ref, woa_ref,
                  cwb_ref, cbb_ref, lng_ref, lnb_ref, wob_ref, bob_ref, wo_ref, fg_ref,
                  out_ref, hbuf, pbuf, ubuf, bzbuf, zgbuf, cbuf, gabuf, gbbuf, mbuf, rsbuf,
                  *, tokens, d_model, n_tiles):
    T, D, CK, RB = tokens, d_model, CHUNK, ROW_BLOCK
    NC = D // CK
    NS = D // LANES
    TH = T + 2 * HALO
    inv_d = 1.0 / D
    bf16 = jnp.bfloat16

    shift = mod_ref[0, 0:1, :]
    scale = mod_ref[0, 1:2, :]
    g1 = ng_ref[...] * (1.0 + scale)

    def norm_rows(x):
        ms = jnp.sum(x * x, axis=-1, keepdims=True) * inv_d
        return (x * lax.rsqrt(ms + EPS) * g1 + shift).astype(bf16)

    hbuf[0:HALO, :] = norm_rows(xp_ref[0])
    hbuf[HALO + T:TH, :] = norm_rows(xn_ref[0])

    def s0(i, carry):
        r0 = pl.multiple_of(i * RB, RB)
        hbuf[pl.ds(pl.multiple_of(r0 + HALO, BF16_SUBLANES), RB), :] = norm_rows(
            xm_ref[0, pl.ds(r0, RB), :])
        return carry

    lax.fori_loop(0, T // RB, s0, 0)

    def put_slabs(buf, c, val):
        for j in range(SLABS_PER_CHUNK):
            buf[c * SLABS_PER_CHUNK + j] = val[:, j * LANES:(j + 1) * LANES]

    def s1(c, carry):
        hall = hbuf[...]
        hmain = hbuf[HALO:HALO + T, :]
        cm = _dot(hall, win_ref[1, c])
        vm = _dot(hall, win_ref[2, c])
        put_slabs(pbuf, c, cm * vm)
        am = _dot(hall, win_ref[4, c])
        gm = _dot(hall, win_ref[5, c])
        put_slabs(ubuf, c, am * _sigmoid(gm))
        ba = _dot(hmain, win_ref[0, c])
        za = _dot(hmain, win_ref[3, c])
        put_slabs(bzbuf, c, ba * (za * _sigmoid(za)))
        zb = _dot(hmain, win_ref[6, c])
        put_slabs(zgbuf, c, zb * _sigmoid(zb))
        return carry

    lax.fori_loop(0, NC, s1, 0)

    ti = pl.program_id(1)

    @pl.when(ti == 0)
    def _():
        z = jnp.zeros((NS, HALO, LANES), jnp.float32)
        pbuf[:, 0:HALO, :] = z
        ubuf[:, 0:HALO, :] = z

    @pl.when(ti == n_tiles - 1)
    def _():
        z = jnp.zeros((NS, HALO, LANES), jnp.float32)
        pbuf[:, HALO + T:TH, :] = z
        ubuf[:, HALO + T:TH, :] = z

    pad_a = (K_A - 1) // 2
    pad_b = (K_B - 1) // 2

    def conv_rows(src, s, w_ref, taps, pad, row0):
        acc = None
        for k in range(taps):
            lo = row0 + HALO - pad + k
            term = src[s, lo:lo + CONV_ROWS, :] * w_ref[s, k:k + 1, :]
            acc = term if acc is None else acc + term
        return acc

    def s2a(s, carry):
        for blk in range(T // CONV_ROWS):
            row0 = blk * CONV_ROWS
            rows = slice(row0, row0 + CONV_ROWS)
            ya = conv_rows(pbuf, s, cwa_ref, K_A, pad_a, row0)
            gabuf[s, rows, :] = (ya * bzbuf[s, rows, :]).astype(bf16)
            cbuf[s, rows, :] = conv_rows(ubuf, s, cwb_ref, K_B, pad_b, row0) + cbb_ref[s]
        return carry

    lax.fori_loop(0, NS, s2a, 0)

    def s2b(rb, carry):
        rows = pl.ds(pl.multiple_of(rb * RB, RB), RB)
        vals = [cbuf[s, rows, :] for s in range(NS)]
        tot = vals[0]
        for s in range(1, NS):
            tot = tot + vals[s]
        mu = tot.sum(axis=-1, keepdims=True) * inv_d
        devs = [v - mu for v in vals]
        sq = devs[0] * devs[0]
        for s in range(1, NS):
            sq = sq + devs[s] * devs[s]
        rstd = lax.rsqrt(sq.sum(axis=-1, keepdims=True) * inv_d + EPS)
        for s in range(NS):
            y = devs[s] * rstd * lng_ref[s] + lnb_ref[s]
            sw = y * _sigmoid(y)
            gbbuf[s, rows, :] = (sw * zgbuf[s, rows, :]).astype(bf16)
        return carry

    lax.fori_loop(0, T // RB, s2b, 0)

    def s3(n, carry):
        hmain = hbuf[HALO:HALO + T, :]
        ga = jnp.concatenate([gabuf[s] for s in range(NS)], axis=1)
        gb = jnp.concatenate([gbbuf[s] for s in range(NS)], axis=1)
        ya = _dot(ga, woa_ref[n])
        yb = _dot(gb, wob_ref[n]) + bob_ref[n]
        ma = _dot(hmain, win_ref[7, n]) + bm_ref[0, n]
        mb = _dot(hmain, win_ref[8, n]) + bm_ref[1, n]
        mbuf[n] = (_sigmoid(ma) * ya + _sigmoid(mb) * yb).astype(bf16)
        return carry

    lax.fori_loop(0, NC, s3, 0)

    gate = mod_ref[0, 2:3, :]
    merged = jnp.concatenate([mbuf[c] for c in range(NC)], axis=1)
    ssq = jnp.zeros((T, 1), jnp.float32)
    for n in range(NC):
        cols = slice(n * CK, (n + 1) * CK)
        r = xm_ref[0, :, cols] + gate[:, cols] * _dot(merged, wo_ref[n])
        out_ref[0, :, cols] = r
        ssq = ssq + (r * r).sum(axis=-1, keepdims=True)
    rsbuf[...] = lax.rsqrt(ssq * inv_d + EPS)

    def s5(i, carry):
        rows = pl.ds(pl.multiple_of(i * RB, RB), RB)
        out_ref[0, rows, :] = out_ref[0, rows, :] * rsbuf[rows, :] * fg_ref[...]
        return carry

    lax.fori_loop(0, T // RB, s5, 0)


def _slabs(v):
    rows, width = v.shape
    return v.reshape(rows, width // LANES, LANES).transpose(1, 0, 2)


def _col_chunks(w, nc):
    k, width = w.shape
    return w.reshape(k, nc, width // nc).transpose(1, 0, 2)


def _resident(shape):
    zeros = (0,) * len(shape)
    return pl.BlockSpec(shape, lambda b, i: zeros, pipeline_mode=pl.Buffered(1))


def kernel(x, c, norm_gain, w_ada, b_ada, w_in, b_merge, conv_a_w, w_out_a, conv_b_w, conv_b_bias,
           ln_b_gain, ln_b_bias, w_out_b, b_out_b, w_o, final_gain):
    batch, seq, d = x.shape
    depth = w_in.shape[0]
    assert depth == 1, "the fused kernel applies the final RMSNorm inside the single layer"
    assert w_in.shape[2] == N_COL_GROUPS * d
    T, CK = TOKENS_PER_STEP, CHUNK
    assert seq % T == 0 and d % CK == 0
    nc = d // CK
    ns = d // LANES
    n_tiles = seq // T
    th = T + 2 * HALO
    bf16 = jnp.bfloat16

    mod = _ada_modulation(c, w_ada[0], b_ada[0]).reshape(batch, 3, d)

    win = w_in[0].astype(bf16).reshape(d, N_COL_GROUPS, nc, CK).transpose(1, 2, 0, 3)
    woa = _col_chunks(w_out_a[0].astype(bf16), nc)
    wob = _col_chunks(w_out_b[0].astype(bf16), nc)
    wo = _col_chunks(w_o[0].astype(bf16), nc)
    bm = b_merge[0].reshape(2, nc, 1, CK)
    bob = _col_chunks(b_out_b, nc)
    cwa = _slabs(conv_a_w[0])
    cwb = _slabs(conv_b_w[0])
    cbb = _slabs(conv_b_bias)
    lng = _slabs(ln_b_gain)
    lnb = _slabs(ln_b_bias)
    ng = norm_gain
    fg = final_gain.reshape(1, d)

    halo_blocks_per_tile = T // HALO
    last_halo_block = seq // HALO - 1

    in_specs = [
        pl.BlockSpec((1, HALO, d), lambda b, i: (b, jnp.maximum(i * halo_blocks_per_tile - 1, 0), 0)),
        pl.BlockSpec((1, T, d), lambda b, i: (b, i, 0)),
        pl.BlockSpec((1, HALO, d),
                     lambda b, i: (b, jnp.minimum((i + 1) * halo_blocks_per_tile, last_halo_block), 0)),
        pl.BlockSpec((1, 3, d), lambda b, i: (b, 0, 0)),
        _resident(ng.shape), _resident(win.shape), _resident(bm.shape), _resident(cwa.shape),
        _resident(woa.shape), _resident(cwb.shape), _resident(cbb.shape), _resident(lng.shape),
        _resident(lnb.shape), _resident(wob.shape), _resident(bob.shape), _resident(wo.shape),
        _resident(fg.shape),
    ]
    scratch = [
        pltpu.VMEM((th, d), bf16),
        pltpu.VMEM((ns, th, LANES), jnp.float32),
        pltpu.VMEM((ns, th, LANES), jnp.float32),
        pltpu.VMEM((ns, T, LANES), jnp.float32),
        pltpu.VMEM((ns, T, LANES), jnp.float32),
        pltpu.VMEM((ns, T, LANES), jnp.float32),
        pltpu.VMEM((ns, T, LANES), bf16),
        pltpu.VMEM((ns, T, LANES), bf16),
        pltpu.VMEM((nc, T, CK), bf16),
        pltpu.VMEM((T, 1), jnp.float32),
    ]
    body = functools.partial(_block_kernel, tokens=T, d_model=d, n_tiles=n_tiles)
    return pl.pallas_call(
        body,
        grid=(batch, n_tiles),
        in_specs=in_specs,
        out_specs=pl.BlockSpec((1, T, d), lambda b, i: (b, i, 0)),
        out_shape=jax.ShapeDtypeStruct((batch, seq, d), x.dtype),
        scratch_shapes=scratch,
        compiler_params=pltpu.CompilerParams(
            dimension_semantics=("arbitrary", "arbitrary"),
            vmem_limit_bytes=VMEM_LIMIT_BYTES),
        name="fused_block",
    )(x, x, x, mod, ng, win, bm, cwa, woa, cwb, cbb, lng, lnb, wob, bob, wo, fg)
```

```python
import functools

import jax
import jax.numpy as jnp
from jax import lax
from jax.experimental import pallas as pl
from jax.experimental.pallas import tpu as pltpu

EPS = 1e-6
K_A = 3
K_B = 31
N_COL_GROUPS = 9
(COL_BA, COL_CA, COL_VA, COL_ZA, COL_AB, COL_GB, COL_ZB, COL_MA, COL_MB) = range(N_COL_GROUPS)

LANES = 128
BF16_SUBLANES = 16
TOKENS_PER_STEP = 512
HALO = 16
CHUNK = 256
SLABS_PER_CHUNK = CHUNK // LANES
CONV_ROWS = 64
ROW_BLOCK = 32
NORM_UNROLL = 4
VMEM_LIMIT_BYTES = 60 * 1024 * 1024

assert HALO >= (K_B - 1) // 2 and HALO % BF16_SUBLANES == 0
assert ROW_BLOCK % BF16_SUBLANES == 0 and TOKENS_PER_STEP % (ROW_BLOCK * NORM_UNROLL) == 0
assert TOKENS_PER_STEP % CONV_ROWS == 0


def _sigmoid(v):
    return jax.nn.sigmoid(v)


def _dot(a, b):
    return jnp.dot(a, b, preferred_element_type=jnp.float32)


def _ada_kernel(c_ref, w_ref, b_ref, o_ref):
    c = c_ref[...]
    act = c * _sigmoid(c)
    o_ref[...] = jnp.dot(act, w_ref[...], preferred_element_type=jnp.float32,
                         precision=lax.Precision.HIGHEST) + b_ref[...]


def _ada_modulation(c, w_ada, b_ada):
    batch, d = c.shape
    n_out = w_ada.shape[1]
    return pl.pallas_call(
        _ada_kernel,
        grid=(n_out // d,),
        in_specs=[
            pl.BlockSpec((batch, d), lambda j: (0, 0)),
            pl.BlockSpec((d, d), lambda j: (0, j)),
            pl.BlockSpec((1, d), lambda j: (0, j)),
        ],
        out_specs=pl.BlockSpec((batch, d), lambda j: (0, j)),
        out_shape=jax.ShapeDtypeStruct((batch, n_out), jnp.float32),
        name="ada_modulation",
    )(c, w_ada, b_ada.reshape(1, n_out))


def _block_kernel(xp_ref, xm_ref, xn_ref, mod_ref, ng_ref, win_ref, bm_ref, cwa_ref, woa_ref,
                  cwb_ref, cbb_ref, lng_ref, lnb_ref, wob_ref, bob_ref, wo_ref, fg_ref,
                  out_ref, hbuf, pbuf, ubuf, zgbuf, cbuf, gabuf, gbbuf, smabuf, smbbuf, mbuf, rsbuf,
                  *, tokens, d_model, n_tiles):
    T, D, CK, RB = tokens, d_model, CHUNK, ROW_BLOCK
    NC = D // CK
    NS = D // LANES
    TH = T + 2 * HALO
    inv_d = 1.0 / D
    bf16 = jnp.bfloat16
    ti = pl.program_id(1)

    shift = mod_ref[0, 0:1, :]
    scale = mod_ref[0, 1:2, :]
    g1 = ng_ref[...] * (1.0 + scale)

    def norm_rows(x):
        ms = jnp.sum(x * x, axis=-1, keepdims=True) * inv_d
        return (x * lax.rsqrt(ms + EPS) * g1 + shift).astype(bf16)

    @pl.when(ti > 0)
    def _():
        hbuf[0:HALO, :] = norm_rows(xp_ref[0])

    @pl.when(ti == 0)
    def _():
        hbuf[0:HALO, :] = jnp.zeros((HALO, D), bf16)

    @pl.when(ti < n_tiles - 1)
    def _():
        hbuf[HALO + T:TH, :] = norm_rows(xn_ref[0])

    @pl.when(ti == n_tiles - 1)
    def _():
        hbuf[HALO + T:TH, :] = jnp.zeros((HALO, D), bf16)

    def s0(i, carry):
        r0 = pl.multiple_of(i * RB, RB)
        hbuf[pl.ds(pl.multiple_of(r0 + HALO, BF16_SUBLANES), RB), :] = norm_rows(
            xm_ref[0, pl.ds(r0, RB), :])
        return carry

    lax.fori_loop(0, T // RB, s0, 0, unroll=NORM_UNROLL)

    pad_a = (K_A - 1) // 2
    pad_b = (K_B - 1) // 2

    def put_slabs(buf, c, val):
        for j in range(SLABS_PER_CHUNK):
            buf[c * SLABS_PER_CHUNK + j] = val[:, j * LANES:(j + 1) * LANES]

    def conv_rows(src, s, w_ref, taps, pad, row0):
        acc = None
        for k in range(taps):
            lo = row0 + HALO - pad + k
            term = src[s, lo:lo + CONV_ROWS, :] * w_ref[s, k:k + 1, :]
            acc = term if acc is None else acc + term
        return acc

    def conv_in_proj(c):
        hall = hbuf[...]
        cm = _dot(hall, win_ref[COL_CA, c])
        vm = _dot(hall, win_ref[COL_VA, c])
        put_slabs(pbuf, c, cm * vm)
        am = _dot(hall, win_ref[COL_AB, c])
        gm = _dot(hall, win_ref[COL_GB, c])
        put_slabs(ubuf, c, am * _sigmoid(gm))

    def conv31_chunk(c):
        for j in range(SLABS_PER_CHUNK):
            s = c * SLABS_PER_CHUNK + j
            for blk in range(T // CONV_ROWS):
                row0 = blk * CONV_ROWS
                cbuf[s, row0:row0 + CONV_ROWS, :] = (
                    conv_rows(ubuf, s, cwb_ref, K_B, pad_b, row0) + cbb_ref[s])

    def layer_norm_rows(row0):
        rows = slice(row0, row0 + RB)
        vals = [cbuf[s, rows, :] for s in range(NS)]
        tot = vals[0]
        for s in range(1, NS):
            tot = tot + vals[s]
        mu = tot.sum(axis=-1, keepdims=True) * inv_d
        devs = [v - mu for v in vals]
        sq = devs[0] * devs[0]
        for s in range(1, NS):
            sq = sq + devs[s] * devs[s]
        rstd = lax.rsqrt(sq.sum(axis=-1, keepdims=True) * inv_d + EPS)
        for s in range(NS):
            y = devs[s] * rstd * lng_ref[s] + lnb_ref[s]
            sw = y * _sigmoid(y)
            gbbuf[s, rows, :] = (sw * zgbuf[s, rows, :]).astype(bf16)

    for c in range(NC):
        conv_in_proj(c)
        if c > 0:
            conv31_chunk(c - 1)

    hmain = hbuf[HALO:HALO + T, :]
    conv31_chunk(NC - 1)
    for n in range(NC):
        zb = _dot(hmain, win_ref[COL_ZB, n])
        put_slabs(zgbuf, n, zb * _sigmoid(zb))
        ma = _dot(hmain, win_ref[COL_MA, n]) + bm_ref[0, n]
        smabuf[n] = _sigmoid(ma)

    ln_blocks = T // RB
    for n in range(NC):
        za = _dot(hmain, win_ref[COL_ZA, n])
        ba = _dot(hmain, win_ref[COL_BA, n])
        bz = ba * (za * _sigmoid(za))
        for j in range(SLABS_PER_CHUNK):
            s = n * SLABS_PER_CHUNK + j
            for blk in range(T // CONV_ROWS):
                row0 = blk * CONV_ROWS
                rows = slice(row0, row0 + CONV_ROWS)
                ya = conv_rows(pbuf, s, cwa_ref, K_A, pad_a, row0)
                gabuf[s, rows, :] = (ya * bz[rows, j * LANES:(j + 1) * LANES]).astype(bf16)
        mb = _dot(hmain, win_ref[COL_MB, n]) + bm_ref[1, n]
        smbbuf[n] = _sigmoid(mb)
        for blk in range(n * ln_blocks // NC, (n + 1) * ln_blocks // NC):
            layer_norm_rows(blk * RB)

    ga = jnp.concatenate([gabuf[s] for s in range(NS)], axis=1)
    gb = jnp.concatenate([gbbuf[s] for s in range(NS)], axis=1)
    for n in range(NC):
        ya = _dot(ga, woa_ref[n])
        yb = _dot(gb, wob_ref[n]) + bob_ref[n]
        mbuf[n] = (smabuf[n] * ya + smbbuf[n] * yb).astype(bf16)

    gate = mod_ref[0, 2:3, :]
    merged = jnp.concatenate([mbuf[c] for c in range(NC)], axis=1)
    ssq = jnp.zeros((T, 1), jnp.float32)
    for n in range(NC):
        cols = slice(n * CK, (n + 1) * CK)
        r = xm_ref[0, :, cols] + gate[:, cols] * _dot(merged, wo_ref[n])
        out_ref[0, :, cols] = r
        ssq = ssq + (r * r).sum(axis=-1, keepdims=True)
    rsbuf[...] = lax.rsqrt(ssq * inv_d + EPS)

    def s5(i, carry):
        rows = pl.ds(pl.multiple_of(i * RB, RB), RB)
        out_ref[0, rows, :] = out_ref[0, rows, :] * rsbuf[rows, :] * fg_ref[...]
        return carry

    lax.fori_loop(0, T // RB, s5, 0, unroll=NORM_UNROLL)


def _slabs(v):
    rows, width = v.shape
    return v.reshape(rows, width // LANES, LANES).transpose(1, 0, 2)


def _col_chunks(w, nc):
    k, width = w.shape
    return w.reshape(k, nc, width // nc).transpose(1, 0, 2)


def _resident(shape):
    zeros = (0,) * len(shape)
    return pl.BlockSpec(shape, lambda b, i: zeros, pipeline_mode=pl.Buffered(1))


def kernel(x, c, norm_gain, w_ada, b_ada, w_in, b_merge, conv_a_w, w_out_a, conv_b_w, conv_b_bias,
           ln_b_gain, ln_b_bias, w_out_b, b_out_b, w_o, final_gain):
    batch, seq, d = x.shape
    depth = w_in.shape[0]
    assert depth == 1, "the fused kernel applies the final RMSNorm inside the single layer"
    assert w_in.shape[2] == N_COL_GROUPS * d
    T, CK = TOKENS_PER_STEP, CHUNK
    assert seq % T == 0 and d % CK == 0
    nc = d // CK
    ns = d // LANES
    n_tiles = seq // T
    th = T + 2 * HALO
    bf16 = jnp.bfloat16

    mod = _ada_modulation(c, w_ada[0], b_ada[0]).reshape(batch, 3, d)

    win = w_in[0].astype(bf16).reshape(d, N_COL_GROUPS, nc, CK).transpose(1, 2, 0, 3)
    woa = _col_chunks(w_out_a[0].astype(bf16), nc)
    wob = _col_chunks(w_out_b[0].astype(bf16), nc)
    wo = _col_chunks(w_o[0].astype(bf16), nc)
    bm = b_merge[0].reshape(2, nc, 1, CK)
    bob = _col_chunks(b_out_b, nc)
    cwa = _slabs(conv_a_w[0])
    cwb = _slabs(conv_b_w[0])
    cbb = _slabs(conv_b_bias)
    lng = _slabs(ln_b_gain)
    lnb = _slabs(ln_b_bias)
    ng = norm_gain
    fg = final_gain.reshape(1, d)

    halo_blocks_per_tile = T // HALO
    last_halo_block = seq // HALO - 1

    in_specs = [
        pl.BlockSpec((1, HALO, d), lambda b, i: (b, jnp.maximum(i * halo_blocks_per_tile - 1, 0), 0)),
        pl.BlockSpec((1, T, d), lambda b, i: (b, i, 0)),
        pl.BlockSpec((1, HALO, d),
                     lambda b, i: (b, jnp.minimum((i + 1) * halo_blocks_per_tile, last_halo_block), 0)),
        pl.BlockSpec((1, 3, d), lambda b, i: (b, 0, 0)),
        _resident(ng.shape), _resident(win.shape), _resident(bm.shape), _resident(cwa.shape),
        _resident(woa.shape), _resident(cwb.shape), _resident(cbb.shape), _resident(lng.shape),
        _resident(lnb.shape), _resident(wob.shape), _resident(bob.shape), _resident(wo.shape),
        _resident(fg.shape),
    ]
    scratch = [
        pltpu.VMEM((th, d), bf16),
        pltpu.VMEM((ns, th, LANES), jnp.float32),
        pltpu.VMEM((ns, th, LANES), jnp.float32),
        pltpu.VMEM((ns, T, LANES), jnp.float32),
        pltpu.VMEM((ns, T, LANES), jnp.float32),
        pltpu.VMEM((ns, T, LANES), bf16),
        pltpu.VMEM((ns, T, LANES), bf16),
        pltpu.VMEM((nc, T, CK), jnp.float32),
        pltpu.VMEM((nc, T, CK), jnp.float32),
        pltpu.VMEM((nc, T, CK), bf16),
        pltpu.VMEM((T, 1), jnp.float32),
    ]
    body = functools.partial(_block_kernel, tokens=T, d_model=d, n_tiles=n_tiles)
    return pl.pallas_call(
        body,
        grid=(batch, n_tiles),
        in_specs=in_specs,
        out_specs=pl.BlockSpec((1, T, d), lambda b, i: (b, i, 0)),
        out_shape=jax.ShapeDtypeStruct((batch, seq, d), x.dtype),
        scratch_shapes=scratch,
        compiler_params=pltpu.CompilerParams(
            dimension_semantics=("arbitrary", "arbitrary"),
            vmem_limit_bytes=VMEM_LIMIT_BYTES),
        name="fused_block",
    )(x, x, x, mod, ng, win, bm, cwa, woa, cwb, cbb, lng, lnb, wob, bob, wo, fg)
```
